```python
import math
import jax, jax.numpy as jnp
from jax import lax
import numpy as np

D_MODEL = 2048
BATCH = 1
SEQ = 8192
DEPTH = 2
DEC_BATCH = 2
DEC_SEQ = 8192
PAST_LEN = 128

GRID_W = 64
HEAD_DIM = 128
AXIS_DIM = HEAD_DIM // 2
ROPE_THETA = 10000.0
EPS = 1e-6
BRANCH_WIDTH = 1024
N_BRANCH = 3
A_Q_HEADS = 8
A_KV_HEADS = 2
A_GROUP = A_Q_HEADS // A_KV_HEADS
Q_BLOCK = 128
R_HEADS = 4
R_QK_DIM = 128
R_V_DIM = BRANCH_WIDTH // R_HEADS
R_CHUNK = 128
C_WIDTH = BRANCH_WIDTH
CONV_W = 3
X_HEADS = 4
X_WIDTH = X_HEADS * HEAD_DIM
N_MEM = 256
D_FF = 5632
N_EXPERTS = 8
TOP_K = 2
D_FF_EXPERT = 2816
N_DENSE = (DEPTH + 1) // 2
N_MOE = DEPTH // 2

SPLIT_SIZES = (A_Q_HEADS * HEAD_DIM, A_KV_HEADS * HEAD_DIM, A_KV_HEADS * HEAD_DIM,
               R_HEADS * R_QK_DIM, R_HEADS * R_QK_DIM, R_HEADS * R_V_DIM, R_HEADS * R_V_DIM,
               C_WIDTH, C_WIDTH, C_WIDTH, N_BRANCH * D_MODEL)
IN_WIDTH = sum(SPLIT_SIZES)

kernel_name = 'hybrid_bidir_encoder_gqa_retention_shortconv_moe'


def _rms_norm(x, g):
    xf = x.astype(jnp.float32)
    y = xf * lax.rsqrt(jnp.mean(xf * xf, axis=-1, keepdims=True) + EPS)
    return (y * g.astype(jnp.float32)).astype(x.dtype)


def _split_in(p):
    out = []
    start = 0
    for size in SPLIT_SIZES:
        out.append(p[..., start:start + size])
        start += size
    return out


def _rope_tables(n):
    rows = n // GRID_W
    row = jnp.repeat(jnp.arange(rows, dtype=jnp.float32), GRID_W)
    col = jnp.tile(jnp.arange(GRID_W, dtype=jnp.float32), rows)
    inv = ROPE_THETA ** (-jnp.arange(0, AXIS_DIM, 2, dtype=jnp.float32) / AXIS_DIM)
    ang_r = row[:, None] * inv[None, :]
    ang_c = col[:, None] * inv[None, :]
    ang = jnp.concatenate([ang_r, ang_r, ang_c, ang_c], axis=-1)
    return jnp.cos(ang), jnp.sin(ang)


def _apply_rope(x, cos, sin):
    xb = x.reshape(x.shape[:-1] + (2, 2, AXIS_DIM // 2))
    rot = jnp.concatenate([-xb[..., 1:, :], xb[..., :1, :]], axis=-2).reshape(x.shape)
    c = cos[:, None, :].astype(x.dtype)
    s = sin[:, None, :].astype(x.dtype)
    return x * c + rot * s


def _gqa_attention(q, k, v, cos, sin, gq, gk):
    B, S = q.shape[0], q.shape[1]
    q = _apply_rope(_rms_norm(q, gq), cos, sin)
    k = _apply_rope(_rms_norm(k, gk), cos, sin)
    scale = HEAD_DIM ** -0.5
    nb = S // Q_BLOCK
    qb = q.reshape(B, nb, Q_BLOCK, A_KV_HEADS, A_GROUP, HEAD_DIM).transpose(1, 0, 2, 3, 4, 5)

    def block(qi):
        s = jnp.einsum('bqhgd,bkhd->bhgqk', qi, k).astype(jnp.float32) * scale
        p = jax.nn.softmax(s, axis=-1)
        return jnp.einsum('bhgqk,bkhd->bqhgd', p.astype(v.dtype), v)

    o = lax.map(block, qb)
    return o.transpose(1, 0, 2, 3, 4, 5).reshape(B, S, A_Q_HEADS * HEAD_DIM)


def _retention_dir(q, k, v, log_g):
    B, S = q.shape[0], q.shape[1]
    N = S // R_CHUNK
    qc = q.reshape(B, N, R_CHUNK, R_HEADS, R_QK_DIM)
    kc = k.reshape(B, N, R_CHUNK, R_HEADS, R_QK_DIM)
    vc = v.reshape(B, N, R_CHUNK, R_HEADS, R_V_DIM)
    pos = jnp.arange(R_CHUNK, dtype=jnp.float32)
    diff = pos[:, None] - pos[None, :]
    dmat = jnp.where(diff[None] >= 0,
                     jnp.exp(jnp.maximum(diff, 0.0)[None] * log_g[:, None, None]), 0.0)
    s = jnp.einsum('bnihd,bnjhd->bnhij', qc, kc) * dmat[None, None]
    intra = jnp.einsum('bnhij,bnjhe->bnihe', s, vc)
    zeta = jnp.exp((R_CHUNK - 1 - pos)[:, None] * log_g[None, :])
    kv = jnp.einsum('bnjhd,jh,bnjhe->nbhde', kc, zeta, vc)
    chunk_decay = jnp.exp(R_CHUNK * log_g)[:, None, None]

    def step(state, kv_n):
        return chunk_decay * state + kv_n, state

    init = jnp.zeros((B, R_HEADS, R_QK_DIM, R_V_DIM), jnp.float32)
    _, states = lax.scan(step, init, kv)
    xi = jnp.exp((pos + 1.0)[:, None] * log_g[None, :])
    cross = jnp.einsum('bnihd,ih,nbhde->bnihe', qc, xi, states)
    return (intra + cross).reshape(B, S, R_HEADS, R_V_DIM)


def _retention(q, k, v, g, cos, sin, decay_logits):
    B, S = q.shape[0], q.shape[1]
    q = _apply_rope(q, cos, sin).astype(jnp.float32)
    k = (_apply_rope(k, cos, sin) * (R_QK_DIM ** -0.5)).astype(jnp.float32)
    v = v.astype(jnp.float32)
    log_g = jax.nn.log_sigmoid(decay_logits.astype(jnp.float32))
    fwd = _retention_dir(q, k, v, log_g[0])
    bwd = _retention_dir(q[:, ::-1], k[:, ::-1], v[:, ::-1], log_g[1])[:, ::-1]
    y = fwd + bwd
    mu = jnp.mean(y, axis=-1, keepdims=True)
    var = jnp.mean(jnp.square(y - mu), axis=-1, keepdims=True)
    y = (y - mu) * lax.rsqrt(var + EPS)
    y = y.reshape(B, S, R_HEADS * R_V_DIM)
    return (jax.nn.silu(g.astype(jnp.float32)) * y).astype(g.dtype)


def _short_conv(bg, cg, h, w, b):
    u = cg * h
    up = jnp.pad(u, ((0, 0), (1, 1), (0, 0)))
    y = up[:, :-2] * w[0] + up[:, 1:-1] * w[1] + up[:, 2:] * w[2] + b
    return bg * y


def _token_mixer(h, cos, sin, w_in, a_qn, a_kn, r_decay, conv_w, conv_b, w_branch, w_out):
    B, S, _ = h.shape
    p = h @ w_in
    aq, ak, av, rq, rk, rv, rg, cb, cc, ch, gates = _split_in(p)
    a = _gqa_attention(aq.reshape(B, S, A_Q_HEADS, HEAD_DIM),
                       ak.reshape(B, S, A_KV_HEADS, HEAD_DIM),
                       av.reshape(B, S, A_KV_HEADS, HEAD_DIM), cos, sin, a_qn, a_kn)
    r = _retention(rq.reshape(B, S, R_HEADS, R_QK_DIM), rk.reshape(B, S, R_HEADS, R_QK_DIM),
                   rv.reshape(B, S, R_HEADS, R_V_DIM), rg, cos, sin, r_decay)
    c = _short_conv(cb, cc, ch, conv_w, conv_b)
    z = jnp.stack([a, r.astype(a.dtype), c], axis=2)
    proj = jnp.einsum('bsnc,ncd->bsnd', z, w_branch)
    gate = jax.nn.sigmoid(gates.reshape(B, S, N_BRANCH, D_MODEL))
    merged = jnp.einsum('bsnd,bsnd->bsd', gate, proj)
    return merged @ w_out


def _cross_attention(h, mem, g_mem, wq, wkv, wo):
    B, S, _ = h.shape
    M = mem.shape[1]
    q = (h @ wq).reshape(B, S, X_HEADS, HEAD_DIM)
    kv = (_rms_norm(mem, g_mem) @ wkv).reshape(B, M, 2, X_HEADS, HEAD_DIM)
    k, v = kv[:, :, 0], kv[:, :, 1]
    s = jnp.einsum('bqhd,bkhd->bhqk', q, k).astype(jnp.float32) * (HEAD_DIM ** -0.5)
    pr = jax.nn.softmax(s, axis=-1)
    o = jnp.einsum('bhqk,bkhd->bqhd', pr.astype(v.dtype), v).reshape(B, S, X_WIDTH)
    return o @ wo


def _swiglu(x, w_in, w_down, f):
    gu = x @ w_in
    return (jax.nn.silu(gu[..., :f]) * gu[..., f:]) @ w_down


def _moe_ffn(h, rw, rb, w_in, w_down):
    B, S, D = h.shape
    t = h.reshape(-1, D)
    logits = (t @ rw).astype(jnp.float32) + rb.astype(jnp.float32)
    top_v, top_i = lax.top_k(logits, TOP_K)
    top_w = jax.nn.softmax(top_v, axis=-1)
    combine = jnp.sum(jax.nn.one_hot(top_i, N_EXPERTS, dtype=jnp.float32) * top_w[..., None], axis=1)
    out = jnp.zeros_like(t)
    for e in range(N_EXPERTS):
        y = _swiglu(t, w_in[e], w_down[e], D_FF_EXPERT)
        out = out + combine[:, e:e + 1].astype(t.dtype) * y
    return out.reshape(B, S, D)


def _forward(x, mem, norm_mix, w_in, a_q_norm, a_k_norm, r_decay, conv_w, conv_b, w_branch, w_out,
             norm_xattn, norm_mem, wq_x, wkv_x, wo_x, norm_ffn, ffn_w_in, ffn_w_down,
             router_w, router_b, moe_w_in, moe_w_down, norm_final):
    S = x.shape[1]
    cos, sin = _rope_tables(S)
    for l in range(DEPTH):
        h = _rms_norm(x, norm_mix[l])
        x = x + _token_mixer(h, cos, sin, w_in[l], a_q_norm[l], a_k_norm[l], r_decay[l],
                             conv_w[l], conv_b[l], w_branch[l], w_out[l])
        h = _rms_norm(x, norm_xattn[l])
        x = x + _cross_attention(h, mem, norm_mem[l], wq_x[l], wkv_x[l], wo_x[l])
        h = _rms_norm(x, norm_ffn[l])
        if l % 2 == 0:
            x = x + _swiglu(h, ffn_w_in[l // 2], ffn_w_down[l // 2], D_FF)
        else:
            x = x + _moe_ffn(h, router_w[l // 2], router_b[l // 2], moe_w_in[l // 2], moe_w_down[l // 2])
    return _rms_norm(x, norm_final)


def setup_inputs(seed: int = 0) -> dict:
    key = jax.random.key(seed)
    ks = jax.random.split(key, 32)
    f32 = jnp.float32

    def nrm(k, shape, scale):
        return jax.random.normal(k, shape, f32) * scale

    def gain(k, shape):
        return 1.0 + 0.1 * jax.random.normal(k, shape, f32)

    base_decay = jnp.log(2.0 ** (5.0 + jnp.arange(R_HEADS, dtype=f32)) - 1.0)
    return {
        'x_prompt': nrm(ks[0], (BATCH, SEQ, D_MODEL), 1.0),
        'x_sample': nrm(ks[1], (DEC_BATCH, DEC_SEQ, D_MODEL), 1.0),
        'mem_prompt': nrm(ks[2], (BATCH, N_MEM, D_MODEL), 1.0),
        'mem_sample': nrm(ks[3], (DEC_BATCH, N_MEM, D_MODEL), 1.0),
        'norm_mix': gain(ks[4], (DEPTH, D_MODEL)),
        'w_in': nrm(ks[5], (DEPTH, D_MODEL, IN_WIDTH), D_MODEL ** -0.5),
        'a_q_norm': gain(ks[6], (DEPTH, HEAD_DIM)),
        'a_k_norm': gain(ks[7], (DEPTH, HEAD_DIM)),
        'r_decay': base_decay[None, None, :] + 0.1 * jax.random.normal(ks[8], (DEPTH, 2, R_HEADS), f32),
        'conv_w': nrm(ks[9], (DEPTH, CONV_W, C_WIDTH), CONV_W ** -0.5),
        'conv_b': nrm(ks[10], (DEPTH, C_WIDTH), 0.01),
        'w_branch': nrm(ks[11], (DEPTH, N_BRANCH, BRANCH_WIDTH, D_MODEL), BRANCH_WIDTH ** -0.5),
        'w_out': nrm(ks[12], (DEPTH, D_MODEL, D_MODEL), D_MODEL ** -0.5),
        'norm_xattn': gain(ks[13], (DEPTH, D_MODEL)),
        'norm_mem': gain(ks[14], (DEPTH, D_MODEL)),
        'wq_x': nrm(ks[15], (DEPTH, D_MODEL, X_WIDTH), D_MODEL ** -0.5),
        'wkv_x': nrm(ks[16], (DEPTH, D_MODEL, 2 * X_WIDTH), D_MODEL ** -0.5),
        'wo_x': nrm(ks[17], (DEPTH, X_WIDTH, D_MODEL), X_WIDTH ** -0.5),
        'norm_ffn': gain(ks[18], (DEPTH, D_MODEL)),
        'ffn_w_in': nrm(ks[19], (N_DENSE, D_MODEL, 2 * D_FF), D_MODEL ** -0.5),
        'ffn_w_down': nrm(ks[20], (N_DENSE, D_FF, D_MODEL), D_FF ** -0.5),
        'router_w': nrm(ks[21], (N_MOE, D_MODEL, N_EXPERTS), D_MODEL ** -0.5),
        'router_b': nrm(ks[22], (N_MOE, N_EXPERTS), 0.01),
        'moe_w_in': nrm(ks[23], (N_MOE, N_EXPERTS, D_MODEL, 2 * D_FF_EXPERT), D_MODEL ** -0.5),
        'moe_w_down': nrm(ks[24], (N_MOE, N_EXPERTS, D_FF_EXPERT, D_MODEL), D_FF_EXPERT ** -0.5),
        'norm_final': gain(ks[25], (D_MODEL,)),
    }


def reference(x_prompt, x_sample, mem_prompt, mem_sample, norm_mix, w_in, a_q_norm, a_k_norm,
              r_decay, conv_w, conv_b, w_branch, w_out, norm_xattn, norm_mem, wq_x, wkv_x, wo_x,
              norm_ffn, ffn_w_in, ffn_w_down, router_w, router_b, moe_w_in, moe_w_down, norm_final):
    y_prompt = _forward(x_prompt, mem_prompt, norm_mix, w_in, a_q_norm, a_k_norm, r_decay, conv_w,
                        conv_b, w_branch, w_out, norm_xattn, norm_mem, wq_x, wkv_x, wo_x, norm_ffn,
                        ffn_w_in, ffn_w_down, router_w, router_b, moe_w_in, moe_w_down, norm_final)
    y_sample = _forward(x_sample, mem_sample, norm_mix, w_in, a_q_norm, a_k_norm, r_decay, conv_w,
                        conv_b, w_branch, w_out, norm_xattn, norm_mem, wq_x, wkv_x, wo_x, norm_ffn,
                        ffn_w_in, ffn_w_down, router_w, router_b, moe_w_in, moe_w_down, norm_final)
    return (y_prompt, y_sample)
```

```python
import functools

import jax
import jax.numpy as jnp
from jax import lax
from jax.experimental import pallas as pl
from jax.experimental.pallas import tpu as pltpu

F32 = jnp.float32
BF16 = jnp.bfloat16

EPS = 1e-6
HEAD_DIM = 128
GRID_W = 64
ROPE_THETA = 10000.0
A_Q_HEADS = 8
A_KV_HEADS = 2
A_GROUP = A_Q_HEADS // A_KV_HEADS
R_HEADS = 4
R_QK_DIM = 128
R_V_DIM = 256
R_CHUNK = 128
BRANCH_WIDTH = 1024
N_BRANCH = 3
X_HEADS = 4
N_EXPERTS = 8
LANES = 128
SUBLANES = 8
VMEM_CAP = 56 * 1024 * 1024


def _cparams(sem, vmem_bytes):
    return pltpu.CompilerParams(dimension_semantics=sem,
                                vmem_limit_bytes=int(min(max(vmem_bytes, 16 << 20), VMEM_CAP)))


def _resident(shape, index_map):
    return pl.BlockSpec(shape, index_map, pipeline_mode=pl.Buffered(1))


def _rms(x, g):
    return x * lax.rsqrt(jnp.mean(x * x, axis=-1, keepdims=True) + EPS) * g


def _rope(y, cos, sin_a, sin_b):
    return y * cos + pltpu.roll(y, 96, 1) * sin_a + pltpu.roll(y, 32, 1) * sin_b


def _dot(a, b):
    return jnp.dot(a, b, preferred_element_type=F32)


def _dot_nt(a, b):
    return lax.dot_general(a, b, (((1,), (1,)), ((), ())), preferred_element_type=F32)


def _sigmoid(x):
    return 1.0 / (1.0 + jnp.exp(-x))


def _p1_kernel(x_ref, g_ref, w_ref, cos_ref, sa_ref, sb_ref, gq_ref, gk_ref,
               qa_ref, ka_ref, va_ref, rq_ref, rk_ref):
    h = _rms(x_ref[...], g_ref[...]).astype(BF16)
    cos, sa, sb = cos_ref[...], sa_ref[...], sb_ref[...]
    gq, gk = gq_ref[...], gk_ref[...]
    a_scale = HEAD_DIM ** -0.5
    r_scale = R_QK_DIM ** -0.5

    def heads(col0, n):
        p = _dot(h, w_ref[:, col0:col0 + n * HEAD_DIM])
        return [p[:, i * HEAD_DIM:(i + 1) * HEAD_DIM] for i in range(n)]

    def put(ref, i, y):
        ref[:, i * HEAD_DIM:(i + 1) * HEAD_DIM] = y.astype(ref.dtype)

    col = 0
    for half in range(2):
        for i, y in enumerate(heads(col, 4)):
            put(qa_ref, half * 4 + i, _rope(_rms(y, gq), cos, sa, sb) * a_scale)
        col += 4 * HEAD_DIM
    kv = heads(col, 4)
    col += 4 * HEAD_DIM
    for i in range(2):
        put(ka_ref, i, _rope(_rms(kv[i], gk), cos, sa, sb))
        put(va_ref, i, kv[2 + i])
    for i, y in enumerate(heads(col, 4)):
        put(rq_ref, i, _rope(y, cos, sa, sb))
    col += 4 * HEAD_DIM
    for i, y in enumerate(heads(col, 4)):
        put(rk_ref, i, _rope(y, cos, sa, sb) * r_scale)


def _p1_call(x, g, w1, cos, sa, sb, gq, gk, seq, tm):
    m, d = x.shape
    n1 = w1.shape[1]
    nt = seq // tm
    row = lambda i: (i, 0)
    fix = lambda i: (0, 0)
    tab = lambda i: (i % nt, 0)
    outs = [(A_Q_HEADS * HEAD_DIM), A_KV_HEADS * HEAD_DIM, A_KV_HEADS * HEAD_DIM,
            R_HEADS * R_QK_DIM, R_HEADS * R_QK_DIM]
    vmem = d * n1 * 2 + 2 * tm * d * 4 + 2 * tm * n1 * 2 + tm * d * 2 + 4 * tm * 512 * 4 + (8 << 20)
    return pl.pallas_call(
        _p1_kernel,
        grid=(m // tm,),
        in_specs=[pl.BlockSpec((tm, d), row), _resident((1, d), fix), _resident((d, n1), fix),
                  pl.BlockSpec((tm, HEAD_DIM), tab), pl.BlockSpec((tm, HEAD_DIM), tab),
                  pl.BlockSpec((tm, HEAD_DIM), tab),
                  _resident((1, HEAD_DIM), fix), _resident((1, HEAD_DIM), fix)],
        out_specs=[pl.BlockSpec((tm, n), row) for n in outs],
        out_shape=[jax.ShapeDtypeStruct((m, n), BF16) for n in outs],
        compiler_params=_cparams(("parallel",), vmem),
        name="p1_inproj_qk",
    )(x, g, w1, cos, sa, sb, gq, gk)


def _p2_kernel(x_ref, g_ref, w_ref, rv_ref, sg_ref, cb_ref, u_ref):
    h = _rms(x_ref[...], g_ref[...]).astype(BF16)
    bw = BRANCH_WIDTH
    cw = 512
    for c in range(bw // cw):
        sl = slice(c * cw, (c + 1) * cw)
        rv_ref[:, sl] = _dot(h, w_ref[:, c * cw:(c + 1) * cw]).astype(BF16)
        gate = _dot(h, w_ref[:, bw + c * cw:bw + (c + 1) * cw])
        sg_ref[:, sl] = (gate * _sigmoid(gate)).astype(BF16)
        cb_ref[:, sl] = _dot(h, w_ref[:, 2 * bw + c * cw:2 * bw + (c + 1) * cw]).astype(BF16)
        cc = _dot(h, w_ref[:, 3 * bw + c * cw:3 * bw + (c + 1) * cw])
        ch = _dot(h, w_ref[:, 4 * bw + c * cw:4 * bw + (c + 1) * cw])
        u_ref[:, sl] = (cc * ch).astype(BF16)


def _p2_call(x, g, w2, tm):
    m, d = x.shape
    n2 = w2.shape[1]
    row = lambda i: (i, 0)
    fix = lambda i: (0, 0)
    vmem = d * n2 * 2 + 2 * tm * d * 4 + 2 * 4 * tm * BRANCH_WIDTH * 2 + tm * d * 2 + 4 * tm * 512 * 4 + (8 << 20)
    return pl.pallas_call(
        _p2_kernel,
        grid=(m // tm,),
        in_specs=[pl.BlockSpec((tm, d), row), _resident((1, d), fix), _resident((d, n2), fix)],
        out_specs=[pl.BlockSpec((tm, BRANCH_WIDTH), row) for _ in range(4)],
        out_shape=[jax.ShapeDtypeStruct((m, BRANCH_WIDTH), BF16) for _ in range(4)],
        compiler_params=_cparams(("parallel",), vmem),
        name="p2_inproj_vgc",
    )(x, g, w2)


def _attn_kernel(q_ref, k_ref, v_ref, o_ref, *, tk):
    tq = q_ref.shape[0]
    seq = k_ref.shape[0]
    q = q_ref[...]
    qs = jnp.concatenate([q[:, g * HEAD_DIM:(g + 1) * HEAD_DIM] for g in range(A_GROUP)], axis=0)
    rows = A_GROUP * tq

    def body(c, carry):
        m_i, l_i, acc = carry
        sl = pl.ds(pl.multiple_of(c * tk, tk), tk)
        s = _dot_nt(qs, k_ref[sl, :])
        m_new = jnp.maximum(m_i, jnp.max(s, axis=-1, keepdims=True))
        alpha = jnp.exp(m_i - m_new)
        p = jnp.exp(s - m_new)
        l_new = alpha * l_i + jnp.sum(p, axis=-1, keepdims=True)
        acc = alpha * acc + _dot(p.astype(BF16), v_ref[sl, :])
        return m_new, l_new, acc

    init = (jnp.full((rows, 1), -jnp.inf, F32), jnp.zeros((rows, 1), F32),
            jnp.zeros((rows, HEAD_DIM), F32))
    _, l_i, acc = lax.fori_loop(0, seq // tk, body, init)
    out = acc / l_i
    for g in range(A_GROUP):
        o_ref[:, g * HEAD_DIM:(g + 1) * HEAD_DIM] = out[g * tq:(g + 1) * tq, :].astype(o_ref.dtype)


def _attn_call(qa, ka, va, nseq, seq, tq, tk):
    m = qa.shape[0]
    nq = seq // tq
    gw = A_GROUP * HEAD_DIM
    vmem = 2 * 2 * seq * HEAD_DIM * 2 + 4 * tq * gw * 2 + 6 * A_GROUP * tq * tk * 4 + (8 << 20)
    return pl.pallas_call(
        functools.partial(_attn_kernel, tk=tk),
        grid=(nseq, A_KV_HEADS, nq),
        in_specs=[pl.BlockSpec((tq, gw), lambda b, h, i: (b * nq + i, h)),
                  pl.BlockSpec((seq, HEAD_DIM), lambda b, h, i: (b, h)),
                  pl.BlockSpec((seq, HEAD_DIM), lambda b, h, i: (b, h))],
        out_specs=pl.BlockSpec((tq, gw), lambda b, h, i: (b * nq + i, h)),
        out_shape=jax.ShapeDtypeStruct((m, A_Q_HEADS * HEAD_DIM), BF16),
        compiler_params=_cparams(("parallel", "parallel", "parallel"), vmem),
        name="attn_gqa",
    )(qa, ka, va)


def _ret_kernel(lg_ref, q_ref, k_ref, v_ref, sg_ref, o_ref, cross_ref):
    seq = q_ref.shape[0]
    c = R_CHUNK
    n = seq // c
    hd = pl.program_id(1)
    lgf = lg_ref[0, hd]
    lgb = lg_ref[1, hd]

    ii = lax.broadcasted_iota(jnp.int32, (c, c), 0).astype(F32)
    jj = lax.broadcasted_iota(jnp.int32, (c, c), 1).astype(F32)
    d = ii - jj
    dsym = jnp.where(d > 0, jnp.exp(d * lgf), jnp.where(d < 0, jnp.exp(-d * lgb), 2.0))
    pos = lax.broadcasted_iota(jnp.int32, (c, 1), 0).astype(F32)
    zeta_f = jnp.exp((c - 1 - pos) * lgf)
    xi_f = jnp.exp((pos + 1.0) * lgf)
    zeta_b = jnp.exp(pos * lgb)
    xi_b = jnp.exp((c - pos) * lgb)
    one = jnp.ones((1, 1), F32)
    dec_f = jnp.exp(one * (c * lgf))
    dec_b = jnp.exp(one * (c * lgb))

    def kv_outer(kz, v):
        return _dot(kz.T.astype(BF16), v)

    def bstep(t, state):
        sl = pl.ds(pl.multiple_of((n - 1 - t) * c, c), c)
        qn = q_ref[sl, :].astype(F32)
        kn = k_ref[sl, :].astype(F32)
        cross_ref[sl, :] = _dot((qn * xi_b).astype(BF16), state.astype(BF16))
        return dec_b * state + kv_outer(kn * zeta_b, v_ref[sl, :])

    lax.fori_loop(0, n, bstep, jnp.zeros((R_QK_DIM, R_V_DIM), F32))

    def fstep(t, state):
        sl = pl.ds(pl.multiple_of(t * c, c), c)
        qb = q_ref[sl, :]
        kb = k_ref[sl, :]
        vn = v_ref[sl, :]
        qn = qb.astype(F32)
        s = _dot_nt(qb, kb) * dsym
        y = _dot(s.astype(BF16), vn)
        y = y + _dot((qn * xi_f).astype(BF16), state.astype(BF16))
        y = y + cross_ref[sl, :]
        mu = jnp.mean(y, axis=-1, keepdims=True)
        yc = y - mu
        var = jnp.mean(yc * yc, axis=-1, keepdims=True)
        yn = yc * lax.rsqrt(var + EPS)
        o_ref[sl, :] = (sg_ref[sl, :].astype(F32) * yn).astype(o_ref.dtype)
        return dec_f * state + kv_outer(kb.astype(F32) * zeta_f, vn)

    lax.fori_loop(0, n, fstep, jnp.zeros((R_QK_DIM, R_V_DIM), F32))


def _ret_call(log_g, rq, rk, rv, sg, nseq, seq):
    m = rq.shape[0]
    qk = lambda b, h: (b, h)
    vmem = 2 * seq * (2 * R_QK_DIM + 3 * R_V_DIM) * 2 + seq * R_V_DIM * 4 + (8 << 20)
    return pl.pallas_call(
        _ret_kernel,
        grid=(nseq, R_HEADS),
        in_specs=[pl.BlockSpec(memory_space=pltpu.SMEM),
                  pl.BlockSpec((seq, R_QK_DIM), qk), pl.BlockSpec((seq, R_QK_DIM), qk),
                  pl.BlockSpec((seq, R_V_DIM), qk), pl.BlockSpec((seq, R_V_DIM), qk)],
        out_specs=pl.BlockSpec((seq, R_V_DIM), qk),
        out_shape=jax.ShapeDtypeStruct((m, R_HEADS * R_V_DIM), BF16),
        scratch_shapes=[pltpu.VMEM((seq, R_V_DIM), F32)],
        compiler_params=_cparams(("parallel", "parallel"), vmem),
        name="retention",
    )(log_g, rq, rk, rv, sg)


def _conv_kernel(u_ref, up_ref, un_ref, b_ref, w_ref, bias_ref, o_ref, *, tiles_per_seq):
    i = pl.program_id(0)
    tm = u_ref.shape[0]
    u = u_ref[...].astype(F32)
    first = (i % tiles_per_seq) == 0
    last = (i % tiles_per_seq) == tiles_per_seq - 1
    prev_row = jnp.where(first, 0.0, up_ref[SUBLANES - 1:SUBLANES, :].astype(F32))
    next_row = jnp.where(last, 0.0, un_ref[0:1, :].astype(F32))
    rid = lax.broadcasted_iota(jnp.int32, u.shape, 0)
    u_prev = jnp.where(rid == 0, prev_row, pltpu.roll(u, 1, 0))
    u_next = jnp.where(rid == tm - 1, next_row, pltpu.roll(u, tm - 1, 0))
    w = w_ref[...]
    y = u_prev * w[0:1, :] + u * w[1:2, :] + u_next * w[2:3, :] + bias_ref[...]
    o_ref[...] = (b_ref[...].astype(F32) * y).astype(o_ref.dtype)


def _conv_call(u, cb, w, bias, seq, tm):
    m, cwid = u.shape
    nt = seq // tm
    rb = tm // SUBLANES
    nblk = m // SUBLANES
    row = lambda i: (i, 0)
    fix = lambda i: (0, 0)
    return pl.pallas_call(
        functools.partial(_conv_kernel, tiles_per_seq=nt),
        grid=(m // tm,),
        in_specs=[pl.BlockSpec((tm, cwid), row),
                  pl.BlockSpec((SUBLANES, cwid), lambda i: (jnp.maximum(i * rb - 1, 0), 0)),
                  pl.BlockSpec((SUBLANES, cwid), lambda i: (jnp.minimum((i + 1) * rb, nblk - 1), 0)),
                  pl.BlockSpec((tm, cwid), row),
                  _resident((SUBLANES, cwid), fix), _resident((1, cwid), fix)],
        out_specs=pl.BlockSpec((tm, cwid), row),
        out_shape=jax.ShapeDtypeStruct((m, cwid), BF16),
        compiler_params=_cparams(("parallel",), 6 * tm * cwid * 4 + (8 << 20)),
        name="short_conv",
    )(u, u, u, cb, w, bias)


def _merge_kernel(x_ref, g_ref, wg_ref, wb_ref, a_ref, r_ref, c_ref, o_ref, h_ref):
    @pl.when(pl.program_id(1) == 0)
    def _():
        h_ref[...] = _rms(x_ref[...], g_ref[...]).astype(BF16)

    h = h_ref[...]
    acc = None
    for nb, z_ref in enumerate((a_ref, r_ref, c_ref)):
        gate = _sigmoid(_dot(h, wg_ref[nb]))
        term = gate * _dot(z_ref[...], wb_ref[nb])
        acc = term if acc is None else acc + term
    o_ref[...] = acc.astype(o_ref.dtype)


def _merge_call(x, g, wg, wb, a, r, c, tm, tn):
    m, d = x.shape
    bw = a.shape[1]
    row = lambda i, j: (i, 0)
    fix = lambda i, j: (0, 0)
    wcol = lambda i, j: (0, 0, j)
    vmem = (2 * tm * d * 4 + tm * d * 2 + 2 * N_BRANCH * (d + bw) * tn * 2 + 2 * N_BRANCH * tm * bw * 2
            + 2 * tm * tn * 2 + 6 * tm * tn * 4 + (8 << 20))
    return pl.pallas_call(
        _merge_kernel,
        grid=(m // tm, d // tn),
        in_specs=[pl.BlockSpec((tm, d), row), _resident((1, d), fix),
                  pl.BlockSpec((N_BRANCH, d, tn), wcol), pl.BlockSpec((N_BRANCH, bw, tn), wcol),
                  pl.BlockSpec((tm, bw), row), pl.BlockSpec((tm, bw), row), pl.BlockSpec((tm, bw), row)],
        out_specs=pl.BlockSpec((tm, tn), lambda i, j: (i, j)),
        out_shape=jax.ShapeDtypeStruct((m, d), BF16),
        scratch_shapes=[pltpu.VMEM((tm, d), BF16)],
        compiler_params=_cparams(("parallel", "arbitrary"), vmem),
        name="merge_gates",
    )(x, g, wg, wb, a, r, c)


def _xkv_kernel(mem_ref, g_ref, w_ref, o_ref):
    h = _rms(mem_ref[...], g_ref[...]).astype(BF16)
    o_ref[...] = _dot(h, w_ref[...]).astype(o_ref.dtype)


def _xkv_call(mem, g, wkv, n_mem):
    m, d = mem.shape
    n = wkv.shape[1]
    return pl.pallas_call(
        _xkv_kernel,
        grid=(m // n_mem,),
        in_specs=[pl.BlockSpec((n_mem, d), lambda i: (i, 0)), _resident((1, d), lambda i: (0, 0)),
                  _resident((d, n), lambda i: (0, 0))],
        out_specs=pl.BlockSpec((n_mem, n), lambda i: (i, 0)),
        out_shape=jax.ShapeDtypeStruct((m, n), BF16),
        compiler_params=_cparams(("parallel",), 4 * n_mem * d * 4 + d * n * 2 + (8 << 20)),
        name="xattn_kv",
    )(mem, g, wkv)


def _outx_kernel(x_ref, mg_ref, wo_ref, g_ref, wq_ref, kv_ref, wox_ref, o_ref):
    x1 = x_ref[...] + _dot(mg_ref[...], wo_ref[...])
    h = _rms(x1, g_ref[...]).astype(BF16)
    xw = X_HEADS * HEAD_DIM
    q = (_dot(h, wq_ref[...]) * (HEAD_DIM ** -0.5)).astype(BF16)
    outs = []
    for hd in range(X_HEADS):
        sl = slice(hd * HEAD_DIM, (hd + 1) * HEAD_DIM)
        s = _dot_nt(q[:, sl], kv_ref[:, sl])
        p = jnp.exp(s - jnp.max(s, axis=-1, keepdims=True))
        o = _dot(p.astype(BF16), kv_ref[:, xw + hd * HEAD_DIM:xw + (hd + 1) * HEAD_DIM])
        outs.append((o / jnp.sum(p, axis=-1, keepdims=True)).astype(BF16))
    o_ref[...] = x1 + _dot(jnp.concatenate(outs, axis=1), wox_ref[...])


def _outx_call(x, merged, w_out, g, wq, kv, wo_x, seq, n_mem, tm):
    m, d = x.shape
    xw = wq.shape[1]
    nt = seq // tm
    row = lambda i: (i, 0)
    fix = lambda i: (0, 0)
    vmem = (4 * tm * d * 4 + 2 * tm * d * 2 + d * d * 2 + 2 * d * xw * 2 + 4 * n_mem * 2 * xw * 2
            + 3 * tm * d * 4 + (8 << 20))
    return pl.pallas_call(
        _outx_kernel,
        grid=(m // tm,),
        in_specs=[pl.BlockSpec((tm, d), row), pl.BlockSpec((tm, d), row), _resident((d, d), fix),
                  _resident((1, d), fix), _resident((d, xw), fix),
                  pl.BlockSpec((n_mem, 2 * xw), lambda i: (i // nt, 0)), _resident((xw, d), fix)],
        out_specs=pl.BlockSpec((tm, d), row),
        out_shape=jax.ShapeDtypeStruct((m, d), F32),
        compiler_params=_cparams(("parallel",), vmem),
        name="outproj_xattn",
    )(x, merged, w_out, g, wq, kv, wo_x)


def _ffn_kernel(x_ref, g_ref, wg_ref, wu_ref, wd_ref, gf_ref, o_ref, h_ref, acc_ref, *, final_norm):
    j = pl.program_id(1)

    @pl.when(j == 0)
    def _():
        h_ref[...] = _rms(x_ref[...], g_ref[...]).astype(BF16)
        acc_ref[...] = jnp.zeros_like(acc_ref)

    h = h_ref[...]
    gate = _dot(h, wg_ref[...])
    act = (gate * _sigmoid(gate) * _dot(h, wu_ref[...])).astype(BF16)
    acc_ref[...] += _dot(act, wd_ref[...])

    @pl.when(j == pl.num_programs(1) - 1)
    def _():
        y = x_ref[...] + acc_ref[...]
        if final_norm:
            y = _rms(y, gf_ref[...])
        o_ref[...] = y


def _ffn_call(x, g, w_in, w_down, g_final, final_norm, tm, tf):
    m, d = x.shape
    f = w_down.shape[0]
    nf = f // tf
    row = lambda i, j: (i, 0)
    fix = lambda i, j: (0, 0)
    vmem = 4 * tm * d * 4 + tm * d * 2 + tm * d * 4 + 2 * 3 * d * tf * 2 + 4 * tm * tf * 4 + (8 << 20)
    return pl.pallas_call(
        functools.partial(_ffn_kernel, final_norm=final_norm),
        grid=(m // tm, nf),
        in_specs=[pl.BlockSpec((tm, d), row), _resident((1, d), fix),
                  pl.BlockSpec((d, tf), lambda i, j: (0, j)),
                  pl.BlockSpec((d, tf), lambda i, j: (0, j + nf)),
                  pl.BlockSpec((tf, d), lambda i, j: (j, 0)),
                  _resident((1, d), fix)],
        out_specs=pl.BlockSpec((tm, d), row),
        out_shape=jax.ShapeDtypeStruct((m, d), F32),
        scratch_shapes=[pltpu.VMEM((tm, d), BF16), pltpu.VMEM((tm, d), F32)],
        compiler_params=_cparams(("parallel", "arbitrary"), vmem),
        name="ffn_swiglu",
    )(x, g, w_in, w_in, w_down, g_final)


def _split_bf16(v):
    hi = v.astype(BF16)
    return hi, (v - hi.astype(F32)).astype(BF16)


def _route(h32, rw_ref, rb_ref):
    h_hi, h_lo = _split_bf16(h32)
    w_hi, w_lo = _split_bf16(rw_ref[...])
    logits = _dot(h_hi, w_hi) + _dot(h_hi, w_lo) + _dot(h_lo, w_hi) + rb_ref[...]
    lane = lax.broadcasted_iota(jnp.int32, logits.shape, 1)
    m1 = jnp.max(logits, axis=-1, keepdims=True)
    i1 = jnp.min(jnp.where(logits == m1, lane, LANES), axis=-1, keepdims=True)
    rest = jnp.where(lane == i1, -jnp.inf, logits)
    m2 = jnp.max(rest, axis=-1, keepdims=True)
    i2 = jnp.min(jnp.where(rest == m2, lane, LANES), axis=-1, keepdims=True)
    e2 = jnp.exp(m2 - m1)
    w1 = 1.0 / (1.0 + e2)
    w2 = e2 / (1.0 + e2)
    return jnp.where(lane == i1, w1, 0.0) + jnp.where(lane == i2, w2, 0.0)


def _moe_kernel(x_ref, g_ref, rw_ref, rb_ref, wg_ref, wu_ref, wd_ref, gf_ref, o_ref,
                h_ref, comb_ref, acc_ref, *, final_norm):
    e = pl.program_id(1)
    j = pl.program_id(2)

    @pl.when((e == 0) & (j == 0))
    def _():
        h32 = _rms(x_ref[...], g_ref[...])
        h_ref[...] = h32.astype(BF16)
        comb_ref[...] = _route(h32, rw_ref, rb_ref)
        acc_ref[...] = jnp.zeros_like(acc_ref)

    h = h_ref[...]
    comb = comb_ref[...]
    lane = lax.broadcasted_iota(jnp.int32, comb.shape, 1)
    ce = jnp.sum(jnp.where(lane == e, comb, 0.0), axis=-1, keepdims=True)
    gate = _dot(h, wg_ref[0])
    act = (gate * _sigmoid(gate) * _dot(h, wu_ref[0])).astype(BF16)
    acc_ref[...] += ce * _dot(act, wd_ref[0])

    @pl.when((e == pl.num_programs(1) - 1) & (j == pl.num_programs(2) - 1))
    def _():
        y = x_ref[...] + acc_ref[...]
        if final_norm:
            y = _rms(y, gf_ref[...])
        o_ref[...] = y


def _moe_call(x, g, rw, rb, w_in, w_down, g_final, final_norm, tm, tf):
    m, d = x.shape
    ne, fe, _ = w_down.shape
    nf = fe // tf
    row = lambda i, e, j: (i, 0)
    fix = lambda i, e, j: (0, 0)
    vmem = (4 * tm * d * 4 + tm * d * 2 + tm * d * 4 + tm * LANES * 4 + 2 * 3 * d * tf * 2
            + 2 * d * LANES * 4 + 4 * tm * tf * 4 + 2 * tm * d * 4 + (8 << 20))
    return pl.pallas_call(
        functools.partial(_moe_kernel, final_norm=final_norm),
        grid=(m // tm, ne, nf),
        in_specs=[pl.BlockSpec((tm, d), row), _resident((1, d), fix),
                  _resident((d, LANES), fix), _resident((1, LANES), fix),
                  pl.BlockSpec((1, d, tf), lambda i, e, j: (e, 0, j)),
                  pl.BlockSpec((1, d, tf), lambda i, e, j: (e, 0, j + nf)),
                  pl.BlockSpec((1, tf, d), lambda i, e, j: (e, j, 0)),
                  _resident((1, d), fix)],
        out_specs=pl.BlockSpec((tm, d), row),
        out_shape=jax.ShapeDtypeStruct((m, d), F32),
        scratch_shapes=[pltpu.VMEM((tm, d), BF16), pltpu.VMEM((tm, LANES), F32), pltpu.VMEM((tm, d), F32)],
        compiler_params=_cparams(("parallel", "arbitrary", "arbitrary"), vmem),
        name="moe_ffn",
    )(x, g, rw, rb, w_in, w_in, w_down, g_final)


def _rope_tables(n):
    rows = n // GRID_W
    axis_dim = HEAD_DIM // 2
    row = jnp.repeat(jnp.arange(rows, dtype=F32), GRID_W)
    col = jnp.tile(jnp.arange(GRID_W, dtype=F32), rows)
    inv = ROPE_THETA ** (-jnp.arange(0, axis_dim, 2, dtype=F32) / axis_dim)
    ang_r = row[:, None] * inv[None, :]
    ang_c = col[:, None] * inv[None, :]
    ang = jnp.concatenate([ang_r, ang_r, ang_c, ang_c], axis=-1)
    cos, sin = jnp.cos(ang), jnp.sin(ang)
    first_half = (jnp.arange(HEAD_DIM) % axis_dim) < (axis_dim // 2)
    sin_a = jnp.where(first_half[None, :], -sin, 0.0)
    sin_b = jnp.where(first_half[None, :], 0.0, sin)
    return cos, sin_a, sin_b


def _pick(total, pref):
    t = min(pref, total)
    while total % t:
        t //= 2
    return t


def kernel(x_prompt, x_sample, mem_prompt, mem_sample, norm_mix, w_in, a_q_norm, a_k_norm, r_decay, conv_w, conv_b, w_branch, w_out, norm_xattn, norm_mem, wq_x, wkv_x, wo_x, norm_ffn, ffn_w_in, ffn_w_down, router_w, router_b, moe_w_in, moe_w_down, norm_final):
    seq, d = x_prompt.shape[1], x_prompt.shape[2]
    n_mem = mem_prompt.shape[1]
    depth = norm_mix.shape[0]
    nb_p = x_prompt.shape[0]
    nseq = nb_p + x_sample.shape[0]
    x = jnp.concatenate([x_prompt.reshape(-1, d), x_sample.reshape(-1, d)], axis=0)
    mem = jnp.concatenate([mem_prompt.reshape(-1, d), mem_sample.reshape(-1, d)], axis=0)
    m = x.shape[0]

    tm = _pick(seq, 512)
    tq = _pick(seq, 256)
    tk = _pick(seq, 512)
    cos, sin_a, sin_b = _rope_tables(seq)
    row2 = lambda v: v.reshape(1, -1).astype(F32)

    n_qk = (A_Q_HEADS + 2 * A_KV_HEADS) * HEAD_DIM + 2 * R_HEADS * R_QK_DIM
    n_vgc = 2 * R_HEADS * R_V_DIM + 3 * BRANCH_WIDTH
    log_g = jax.nn.log_sigmoid(r_decay.astype(F32))

    for l in range(depth):
        wl = w_in[l]
        w1 = wl[:, :n_qk].astype(BF16)
        w2 = wl[:, n_qk:n_qk + n_vgc].astype(BF16)
        wg = wl[:, n_qk + n_vgc:].reshape(d, N_BRANCH, d).transpose(1, 0, 2).astype(BF16)
        g_mix = row2(norm_mix[l])

        qa, ka, va, rq, rk = _p1_call(x, g_mix, w1, cos, sin_a, sin_b, row2(a_q_norm[l]),
                                      row2(a_k_norm[l]), seq, tm)
        rv, sg, cb, u = _p2_call(x, g_mix, w2, _pick(seq, 256))
        a = _attn_call(qa, ka, va, nseq, seq, tq, tk)
        r = _ret_call(log_g[l], rq, rk, rv, sg, nseq, seq)
        cw = jnp.zeros((SUBLANES, BRANCH_WIDTH), F32).at[:conv_w.shape[1]].set(conv_w[l].astype(F32))
        c = _conv_call(u, cb, cw, row2(conv_b[l]), seq, _pick(seq, 1024))
        merged = _merge_call(x, g_mix, wg, w_branch[l].astype(BF16), a, r, c, tm, _pick(d, 512))

        kv = _xkv_call(mem, row2(norm_mem[l]), wkv_x[l].astype(BF16), n_mem)
        x = _outx_call(x, merged, w_out[l].astype(BF16), row2(norm_xattn[l]), wq_x[l].astype(BF16),
                       kv, wo_x[l].astype(BF16), seq, n_mem, tm)

        last = l == depth - 1
        g_ffn = row2(norm_ffn[l])
        g_fin = row2(norm_final)
        if l % 2 == 0:
            x = _ffn_call(x, g_ffn, ffn_w_in[l // 2].astype(BF16), ffn_w_down[l // 2].astype(BF16),
                          g_fin, last, tm, 512)
        else:
            rw = jnp.zeros((d, LANES), F32).at[:, :N_EXPERTS].set(router_w[l // 2].astype(F32))
            rb = jnp.full((1, LANES), -jnp.inf, F32).at[0, :N_EXPERTS].set(router_b[l // 2].astype(F32))
            x = _moe_call(x, g_ffn, rw, rb, moe_w_in[l // 2].astype(BF16), moe_w_down[l // 2].astype(BF16),
                          g_fin, last, tm, 256)

    split = nb_p * seq
    return (x[:split].reshape(x_prompt.shape), x[split:].reshape(x_sample.shape))
```

```python
import functools

import jax
import jax.numpy as jnp
from jax import lax
from jax.experimental import pallas as pl
from jax.experimental.pallas import tpu as pltpu

F32 = jnp.float32
BF16 = jnp.bfloat16

EPS = 1e-6
LOG2_E = 1.4426950408889634
HEAD_DIM = 128
GRID_W = 64
ROPE_THETA = 10000.0
A_Q_HEADS = 8
A_KV_HEADS = 2
A_GROUP = A_Q_HEADS // A_KV_HEADS
R_HEADS = 4
R_QK_DIM = 128
R_V_DIM = 256
R_CHUNK = 128
BRANCH_WIDTH = 1024
N_BRANCH = 3
X_HEADS = 4
N_EXPERTS = 8
LANES = 128
SUBLANES = 8
VMEM_CAP = 56 * 1024 * 1024


def _cparams(sem, vmem_bytes):
    return pltpu.CompilerParams(dimension_semantics=sem,
                                vmem_limit_bytes=int(min(max(vmem_bytes, 16 << 20), VMEM_CAP)))


def _resident(shape, index_map):
    return pl.BlockSpec(shape, index_map, pipeline_mode=pl.Buffered(1))


def _rms(x, g):
    return x * lax.rsqrt(jnp.mean(x * x, axis=-1, keepdims=True) + EPS) * g


def _rope(y, cos, sin_a, sin_b):
    return y * cos + pltpu.roll(y, 96, 1) * sin_a + pltpu.roll(y, 32, 1) * sin_b


def _dot(a, b):
    return jnp.dot(a, b, preferred_element_type=F32)


def _dot_nt(a, b):
    return lax.dot_general(a, b, (((1,), (1,)), ((), ())), preferred_element_type=F32)


def _sigmoid(x):
    return 1.0 / (1.0 + jnp.exp(-x))


def _p1_kernel(x_ref, g_ref, w_ref, cos_ref, sa_ref, sb_ref, gq_ref, gk_ref,
               qa_ref, ka_ref, va_ref, rq_ref, rk_ref):
    h = _rms(x_ref[...], g_ref[...]).astype(BF16)
    cos, sa, sb = cos_ref[...], sa_ref[...], sb_ref[...]
    gq, gk = gq_ref[...], gk_ref[...]
    a_scale = HEAD_DIM ** -0.5 * LOG2_E
    r_scale = R_QK_DIM ** -0.5

    def heads(col0, n):
        p = _dot(h, w_ref[:, col0:col0 + n * HEAD_DIM])
        return [p[:, i * HEAD_DIM:(i + 1) * HEAD_DIM] for i in range(n)]

    def put(ref, i, y):
        ref[:, i * HEAD_DIM:(i + 1) * HEAD_DIM] = y.astype(ref.dtype)

    col = 0
    for half in range(2):
        for i, y in enumerate(heads(col, 4)):
            put(qa_ref, half * 4 + i, _rope(_rms(y, gq), cos, sa, sb) * a_scale)
        col += 4 * HEAD_DIM
    kv = heads(col, 4)
    col += 4 * HEAD_DIM
    for i in range(2):
        put(ka_ref, i, _rope(_rms(kv[i], gk), cos, sa, sb))
        put(va_ref, i, kv[2 + i])
    for i, y in enumerate(heads(col, 4)):
        put(rq_ref, i, _rope(y, cos, sa, sb))
    col += 4 * HEAD_DIM
    for i, y in enumerate(heads(col, 4)):
        put(rk_ref, i, _rope(y, cos, sa, sb) * r_scale)


def _p1_call(x, g, w1, cos, sa, sb, gq, gk, seq, tm):
    m, d = x.shape
    n1 = w1.shape[1]
    nt = seq // tm
    row = lambda i: (i, 0)
    fix = lambda i: (0, 0)
    tab = lambda i: (i % nt, 0)
    outs = [(A_Q_HEADS * HEAD_DIM), A_KV_HEADS * HEAD_DIM, A_KV_HEADS * HEAD_DIM,
            R_HEADS * R_QK_DIM, R_HEADS * R_QK_DIM]
    vmem = d * n1 * 2 + 2 * tm * d * 4 + 2 * tm * n1 * 2 + tm * d * 2 + 4 * tm * 512 * 4 + (8 << 20)
    return pl.pallas_call(
        _p1_kernel,
        grid=(m // tm,),
        in_specs=[pl.BlockSpec((tm, d), row), _resident((1, d), fix), _resident((d, n1), fix),
                  pl.BlockSpec((tm, HEAD_DIM), tab), pl.BlockSpec((tm, HEAD_DIM), tab),
                  pl.BlockSpec((tm, HEAD_DIM), tab),
                  _resident((1, HEAD_DIM), fix), _resident((1, HEAD_DIM), fix)],
        out_specs=[pl.BlockSpec((tm, n), row) for n in outs],
        out_shape=[jax.ShapeDtypeStruct((m, n), BF16) for n in outs],
        compiler_params=_cparams(("parallel",), vmem),
        name="p1_inproj_qk",
    )(x, g, w1, cos, sa, sb, gq, gk)


def _p2_kernel(x_ref, g_ref, w_ref, rv_ref, sg_ref, cb_ref, u_ref):
    h = _rms(x_ref[...], g_ref[...]).astype(BF16)
    bw = BRANCH_WIDTH
    cw = 512
    for c in range(bw // cw):
        sl = slice(c * cw, (c + 1) * cw)
        rv_ref[:, sl] = _dot(h, w_ref[:, c * cw:(c + 1) * cw]).astype(BF16)
        gate = _dot(h, w_ref[:, bw + c * cw:bw + (c + 1) * cw])
        sg_ref[:, sl] = (gate * _sigmoid(gate)).astype(BF16)
        cb_ref[:, sl] = _dot(h, w_ref[:, 2 * bw + c * cw:2 * bw + (c + 1) * cw]).astype(BF16)
        cc = _dot(h, w_ref[:, 3 * bw + c * cw:3 * bw + (c + 1) * cw])
        ch = _dot(h, w_ref[:, 4 * bw + c * cw:4 * bw + (c + 1) * cw])
        u_ref[:, sl] = (cc * ch).astype(BF16)


def _p2_call(x, g, w2, tm):
    m, d = x.shape
    n2 = w2.shape[1]
    row = lambda i: (i, 0)
    fix = lambda i: (0, 0)
    vmem = d * n2 * 2 + 2 * tm * d * 4 + 2 * 4 * tm * BRANCH_WIDTH * 2 + tm * d * 2 + 4 * tm * 512 * 4 + (8 << 20)
    return pl.pallas_call(
        _p2_kernel,
        grid=(m // tm,),
        in_specs=[pl.BlockSpec((tm, d), row), _resident((1, d), fix), _resident((d, n2), fix)],
        out_specs=[pl.BlockSpec((tm, BRANCH_WIDTH), row) for _ in range(4)],
        out_shape=[jax.ShapeDtypeStruct((m, BRANCH_WIDTH), BF16) for _ in range(4)],
        compiler_params=_cparams(("parallel",), vmem),
        name="p2_inproj_vgc",
    )(x, g, w2)


def _attn_kernel(q_ref, k_ref, v_ref, o_ref, vt_ref, m_ref, l_ref, acc_ref,
                 s0_ref, s1_ref, p0_ref, p1_ref, a0_ref, a1_ref, *, tk):
    tq = q_ref.shape[0]
    seq = k_ref.shape[0]
    n = seq // tk

    @pl.when(pl.program_id(2) == 0)
    def _():
        vt_ref[...] = v_ref[...].astype(F32).T.astype(BF16)

    q = q_ref[...]
    qs = jnp.concatenate([q[:, g * HEAD_DIM:(g + 1) * HEAD_DIM] for g in range(A_GROUP)], axis=0)
    m_ref[...] = jnp.full(m_ref.shape, -jnp.inf, F32)
    l_ref[...] = jnp.zeros(l_ref.shape, F32)
    acc_ref[...] = jnp.zeros(acc_ref.shape, F32)
    s_buf, p_buf, a_buf = (s0_ref, s1_ref), (p0_ref, p1_ref), (a0_ref, a1_ref)

    def chunk(c):
        return pl.ds(pl.multiple_of(c * tk, tk), tk)

    def qk(c, par):
        s_buf[par][...] = _dot_nt(k_ref[chunk(c), :], qs)

    def softmax(par):
        st = s_buf[par][...]
        m_old = m_ref[...]
        m_new = jnp.maximum(m_old, jnp.max(st, axis=0, keepdims=True))
        alpha = jnp.exp2(m_old - m_new)
        p = jnp.exp2(st - m_new)
        l_ref[...] = alpha * l_ref[...] + jnp.sum(p, axis=0, keepdims=True)
        m_ref[...] = m_new
        a_buf[par][...] = alpha
        p_buf[par][...] = p.astype(BF16)

    def pv(c, par):
        acc_ref[...] = a_buf[par][...] * acc_ref[...] + _dot(vt_ref[:, chunk(c)], p_buf[par][...])

    qk(0, 0)
    qk(1, 1)
    softmax(0)

    def body(j, carry):
        c = 1 + 2 * j
        qk(c + 1, 0)
        softmax(1)
        pv(c - 1, 0)
        qk(c + 2, 1)
        softmax(0)
        pv(c, 1)
        return carry

    lax.fori_loop(0, (n - 2) // 2, body, 0)
    softmax(1)
    pv(n - 2, 0)
    pv(n - 1, 1)
    out = acc_ref[...] / l_ref[...]
    for g in range(A_GROUP):
        o_ref[:, g * HEAD_DIM:(g + 1) * HEAD_DIM] = out[:, g * tq:(g + 1) * tq].T.astype(o_ref.dtype)


def _attn_call(qa, ka, va, nseq, seq, tq, tk):
    m = qa.shape[0]
    nq = seq // tq
    gw = A_GROUP * HEAD_DIM
    cols = A_GROUP * tq
    assert (seq // tk) % 2 == 0 and seq // tk >= 2
    vmem = (2 * 2 * seq * HEAD_DIM * 2 + seq * HEAD_DIM * 2 + 4 * tq * gw * 2 + 7 * cols * tk * 4
            + 2 * HEAD_DIM * cols * 4 + seq * HEAD_DIM * 4 + (8 << 20))
    return pl.pallas_call(
        functools.partial(_attn_kernel, tk=tk),
        grid=(nseq, A_KV_HEADS, nq),
        in_specs=[pl.BlockSpec((tq, gw), lambda b, h, i: (b * nq + i, h)),
                  pl.BlockSpec((seq, HEAD_DIM), lambda b, h, i: (b, h)),
                  pl.BlockSpec((seq, HEAD_DIM), lambda b, h, i: (b, h))],
        out_specs=pl.BlockSpec((tq, gw), lambda b, h, i: (b * nq + i, h)),
        out_shape=jax.ShapeDtypeStruct((m, A_Q_HEADS * HEAD_DIM), BF16),
        scratch_shapes=[pltpu.VMEM((HEAD_DIM, seq), BF16), pltpu.VMEM((1, cols), F32),
                        pltpu.VMEM((1, cols), F32), pltpu.VMEM((HEAD_DIM, cols), F32),
                        pltpu.VMEM((tk, cols), F32), pltpu.VMEM((tk, cols), F32),
                        pltpu.VMEM((tk, cols), BF16), pltpu.VMEM((tk, cols), BF16),
                        pltpu.VMEM((1, cols), F32), pltpu.VMEM((1, cols), F32)],
        compiler_params=_cparams(("parallel", "parallel", "arbitrary"), vmem),
        name="attn_gqa",
    )(qa, ka, va)


def _ret_kernel(lg_ref, q_ref, k_ref, v_ref, sg_ref, o_ref, cross_ref):
    seq = q_ref.shape[0]
    c = R_CHUNK
    n = seq // c
    hd = pl.program_id(1)
    lgf = lg_ref[0, hd]
    lgb = lg_ref[1, hd]

    ii = lax.broadcasted_iota(jnp.int32, (c, c), 0).astype(F32)
    jj = lax.broadcasted_iota(jnp.int32, (c, c), 1).astype(F32)
    d = ii - jj
    dsym = jnp.where(d > 0, jnp.exp(d * lgf), jnp.where(d < 0, jnp.exp(-d * lgb), 2.0))
    pos = lax.broadcasted_iota(jnp.int32, (c, 1), 0).astype(F32)
    zeta_f = jnp.exp((c - 1 - pos) * lgf)
    xi_f = jnp.exp((pos + 1.0) * lgf)
    zeta_b = jnp.exp(pos * lgb)
    xi_b = jnp.exp((c - pos) * lgb)
    one = jnp.ones((1, 1), F32)
    dec_f = jnp.exp(one * (c * lgf))
    dec_b = jnp.exp(one * (c * lgb))

    def kv_outer(kz, v):
        return _dot(kz.T.astype(BF16), v)

    def bstep(t, state):
        sl = pl.ds(pl.multiple_of((n - 1 - t) * c, c), c)
        qn = q_ref[sl, :].astype(F32)
        kn = k_ref[sl, :].astype(F32)
        cross_ref[sl, :] = _dot((qn * xi_b).astype(BF16), state.astype(BF16))
        return dec_b * state + kv_outer(kn * zeta_b, v_ref[sl, :])

    lax.fori_loop(0, n, bstep, jnp.zeros((R_QK_DIM, R_V_DIM), F32))

    def fstep(t, state):
        sl = pl.ds(pl.multiple_of(t * c, c), c)
        qb = q_ref[sl, :]
        kb = k_ref[sl, :]
        vn = v_ref[sl, :]
        qn = qb.astype(F32)
        s = _dot_nt(qb, kb) * dsym
        y = _dot(s.astype(BF16), vn)
        y = y + _dot((qn * xi_f).astype(BF16), state.astype(BF16))
        y = y + cross_ref[sl, :]
        mu = jnp.mean(y, axis=-1, keepdims=True)
        yc = y - mu
        var = jnp.mean(yc * yc, axis=-1, keepdims=True)
        yn = yc * lax.rsqrt(var + EPS)
        o_ref[sl, :] = (sg_ref[sl, :].astype(F32) * yn).astype(o_ref.dtype)
        return dec_f * state + kv_outer(kb.astype(F32) * zeta_f, vn)

    lax.fori_loop(0, n, fstep, jnp.zeros((R_QK_DIM, R_V_DIM), F32))


def _ret_call(log_g, rq, rk, rv, sg, nseq, seq):
    m = rq.shape[0]
    qk = lambda b, h: (b, h)
    vmem = 2 * seq * (2 * R_QK_DIM + 3 * R_V_DIM) * 2 + seq * R_V_DIM * 4 + (8 << 20)
    return pl.pallas_call(
        _ret_kernel,
        grid=(nseq, R_HEADS),
        in_specs=[pl.BlockSpec(memory_space=pltpu.SMEM),
                  pl.BlockSpec((seq, R_QK_DIM), qk), pl.BlockSpec((seq, R_QK_DIM), qk),
                  pl.BlockSpec((seq, R_V_DIM), qk), pl.BlockSpec((seq, R_V_DIM), qk)],
        out_specs=pl.BlockSpec((seq, R_V_DIM), qk),
        out_shape=jax.ShapeDtypeStruct((m, R_HEADS * R_V_DIM), BF16),
        scratch_shapes=[pltpu.VMEM((seq, R_V_DIM), F32)],
        compiler_params=_cparams(("parallel", "parallel"), vmem),
        name="retention",
    )(log_g, rq, rk, rv, sg)


def _conv_kernel(u_ref, up_ref, un_ref, b_ref, w_ref, bias_ref, o_ref, *, tiles_per_seq):
    i = pl.program_id(0)
    tm = u_ref.shape[0]
    u = u_ref[...].astype(F32)
    first = (i % tiles_per_seq) == 0
    last = (i % tiles_per_seq) == tiles_per_seq - 1
    prev_row = jnp.where(first, 0.0, up_ref[SUBLANES - 1:SUBLANES, :].astype(F32))
    next_row = jnp.where(last, 0.0, un_ref[0:1, :].astype(F32))
    rid = lax.broadcasted_iota(jnp.int32, u.shape, 0)
    u_prev = jnp.where(rid == 0, prev_row, pltpu.roll(u, 1, 0))
    u_next = jnp.where(rid == tm - 1, next_row, pltpu.roll(u, tm - 1, 0))
    w = w_ref[...]
    y = u_prev * w[0:1, :] + u * w[1:2, :] + u_next * w[2:3, :] + bias_ref[...]
    o_ref[...] = (b_ref[...].astype(F32) * y).astype(o_ref.dtype)


def _conv_call(u, cb, w, bias, seq, tm):
    m, cwid = u.shape
    nt = seq // tm
    rb = tm // SUBLANES
    nblk = m // SUBLANES
    row = lambda i: (i, 0)
    fix = lambda i: (0, 0)
    return pl.pallas_call(
        functools.partial(_conv_kernel, tiles_per_seq=nt),
        grid=(m // tm,),
        in_specs=[pl.BlockSpec((tm, cwid), row),
                  pl.BlockSpec((SUBLANES, cwid), lambda i: (jnp.maximum(i * rb - 1, 0), 0)),
                  pl.BlockSpec((SUBLANES, cwid), lambda i: (jnp.minimum((i + 1) * rb, nblk - 1), 0)),
                  pl.BlockSpec((tm, cwid), row),
                  _resident((SUBLANES, cwid), fix), _resident((1, cwid), fix)],
        out_specs=pl.BlockSpec((tm, cwid), row),
        out_shape=jax.ShapeDtypeStruct((m, cwid), BF16),
        compiler_params=_cparams(("parallel",), 6 * tm * cwid * 4 + (8 << 20)),
        name="short_conv",
    )(u, u, u, cb, w, bias)


def _merge_kernel(x_ref, g_ref, wg_ref, wb_ref, a_ref, r_ref, c_ref, o_ref, h_ref):
    @pl.when(pl.program_id(1) == 0)
    def _():
        h_ref[...] = _rms(x_ref[...], g_ref[...]).astype(BF16)

    h = h_ref[...]
    acc = None
    for nb, z_ref in enumerate((a_ref, r_ref, c_ref)):
        gate = _sigmoid(_dot(h, wg_ref[nb]))
        term = gate * _dot(z_ref[...], wb_ref[nb])
        acc = term if acc is None else acc + term
    o_ref[...] = acc.astype(o_ref.dtype)


def _merge_call(x, g, wg, wb, a, r, c, tm, tn):
    m, d = x.shape
    bw = a.shape[1]
    row = lambda i, j: (i, 0)
    fix = lambda i, j: (0, 0)
    wcol = lambda i, j: (0, 0, j)
    vmem = (2 * tm * d * 4 + tm * d * 2 + 2 * N_BRANCH * (d + bw) * tn * 2 + 2 * N_BRANCH * tm * bw * 2
            + 2 * tm * tn * 2 + 6 * tm * tn * 4 + (8 << 20))
    return pl.pallas_call(
        _merge_kernel,
        grid=(m // tm, d // tn),
        in_specs=[pl.BlockSpec((tm, d), row), _resident((1, d), fix),
                  pl.BlockSpec((N_BRANCH, d, tn), wcol), pl.BlockSpec((N_BRANCH, bw, tn), wcol),
                  pl.BlockSpec((tm, bw), row), pl.BlockSpec((tm, bw), row), pl.BlockSpec((tm, bw), row)],
        out_specs=pl.BlockSpec((tm, tn), lambda i, j: (i, j)),
        out_shape=jax.ShapeDtypeStruct((m, d), BF16),
        scratch_shapes=[pltpu.VMEM((tm, d), BF16)],
        compiler_params=_cparams(("parallel", "arbitrary"), vmem),
        name="merge_gates",
    )(x, g, wg, wb, a, r, c)


def _xkv_kernel(mem_ref, g_ref, w_ref, o_ref):
    h = _rms(mem_ref[...], g_ref[...]).astype(BF16)
    o_ref[...] = _dot(h, w_ref[...]).astype(o_ref.dtype)


def _xkv_call(mem, g, wkv, n_mem):
    m, d = mem.shape
    n = wkv.shape[1]
    return pl.pallas_call(
        _xkv_kernel,
        grid=(m // n_mem,),
        in_specs=[pl.BlockSpec((n_mem, d), lambda i: (i, 0)), _resident((1, d), lambda i: (0, 0)),
                  _resident((d, n), lambda i: (0, 0))],
        out_specs=pl.BlockSpec((n_mem, n), lambda i: (i, 0)),
        out_shape=jax.ShapeDtypeStruct((m, n), BF16),
        compiler_params=_cparams(("parallel",), 4 * n_mem * d * 4 + d * n * 2 + (8 << 20)),
        name="xattn_kv",
    )(mem, g, wkv)


def _outx_kernel(x_ref, mg_ref, wo_ref, g_ref, wq_ref, kv_ref, wox_ref, o_ref):
    x1 = x_ref[...] + _dot(mg_ref[...], wo_ref[...])
    h = _rms(x1, g_ref[...]).astype(BF16)
    xw = X_HEADS * HEAD_DIM
    q = (_dot(h, wq_ref[...]) * (HEAD_DIM ** -0.5)).astype(BF16)
    outs = []
    for hd in range(X_HEADS):
        sl = slice(hd * HEAD_DIM, (hd + 1) * HEAD_DIM)
        s = _dot_nt(q[:, sl], kv_ref[:, sl])
        p = jnp.exp(s - jnp.max(s, axis=-1, keepdims=True))
        o = _dot(p.astype(BF16), kv_ref[:, xw + hd * HEAD_DIM:xw + (hd + 1) * HEAD_DIM])
        outs.append((o / jnp.sum(p, axis=-1, keepdims=True)).astype(BF16))
    o_ref[...] = x1 + _dot(jnp.concatenate(outs, axis=1), wox_ref[...])


def _outx_call(x, merged, w_out, g, wq, kv, wo_x, seq, n_mem, tm):
    m, d = x.shape
    xw = wq.shape[1]
    nt = seq // tm
    row = lambda i: (i, 0)
    fix = lambda i: (0, 0)
    vmem = (4 * tm * d * 4 + 2 * tm * d * 2 + d * d * 2 + 2 * d * xw * 2 + 4 * n_mem * 2 * xw * 2
            + 3 * tm * d * 4 + (8 << 20))
    return pl.pallas_call(
        _outx_kernel,
        grid=(m // tm,),
        in_specs=[pl.BlockSpec((tm, d), row), pl.BlockSpec((tm, d), row), _resident((d, d), fix),
                  _resident((1, d), fix), _resident((d, xw), fix),
                  pl.BlockSpec((n_mem, 2 * xw), lambda i: (i // nt, 0)), _resident((xw, d), fix)],
        out_specs=pl.BlockSpec((tm, d), row),
        out_shape=jax.ShapeDtypeStruct((m, d), F32),
        compiler_params=_cparams(("parallel",), vmem),
        name="outproj_xattn",
    )(x, merged, w_out, g, wq, kv, wo_x)


def _ffn_kernel(x_ref, g_ref, wg_ref, wu_ref, wd_ref, gf_ref, o_ref, h_ref, acc_ref, *, final_norm):
    j = pl.program_id(1)

    @pl.when(j == 0)
    def _():
        h_ref[...] = _rms(x_ref[...], g_ref[...]).astype(BF16)
        acc_ref[...] = jnp.zeros_like(acc_ref)

    h = h_ref[...]
    gate = _dot(h, wg_ref[...])
    act = (gate * _sigmoid(gate) * _dot(h, wu_ref[...])).astype(BF16)
    acc_ref[...] += _dot(act, wd_ref[...])

    @pl.when(j == pl.num_programs(1) - 1)
    def _():
        y = x_ref[...] + acc_ref[...]
        if final_norm:
            y = _rms(y, gf_ref[...])
        o_ref[...] = y


def _ffn_call(x, g, w_in, w_down, g_final, final_norm, tm, tf):
    m, d = x.shape
    f = w_down.shape[0]
    nf = f // tf
    row = lambda i, j: (i, 0)
    fix = lambda i, j: (0, 0)
    vmem = 4 * tm * d * 4 + tm * d * 2 + tm * d * 4 + 2 * 3 * d * tf * 2 + 4 * tm * tf * 4 + (8 << 20)
    return pl.pallas_call(
        functools.partial(_ffn_kernel, final_norm=final_norm),
        grid=(m // tm, nf),
        in_specs=[pl.BlockSpec((tm, d), row), _resident((1, d), fix),
                  pl.BlockSpec((d, tf), lambda i, j: (0, j)),
                  pl.BlockSpec((d, tf), lambda i, j: (0, j + nf)),
                  pl.BlockSpec((tf, d), lambda i, j: (j, 0)),
                  _resident((1, d), fix)],
        out_specs=pl.BlockSpec((tm, d), row),
        out_shape=jax.ShapeDtypeStruct((m, d), F32),
        scratch_shapes=[pltpu.VMEM((tm, d), BF16), pltpu.VMEM((tm, d), F32)],
        compiler_params=_cparams(("parallel", "arbitrary"), vmem),
        name="ffn_swiglu",
    )(x, g, w_in, w_in, w_down, g_final)


def _split_bf16(v):
    hi = v.astype(BF16)
    return hi, (v - hi.astype(F32)).astype(BF16)


def _route(h32, rw_ref, rb_ref):
    h_hi, h_lo = _split_bf16(h32)
    w_hi, w_lo = _split_bf16(rw_ref[...])
    logits = _dot(h_hi, w_hi) + _dot(h_hi, w_lo) + _dot(h_lo, w_hi) + rb_ref[...]
    lane = lax.broadcasted_iota(jnp.int32, logits.shape, 1)
    m1 = jnp.max(logits, axis=-1, keepdims=True)
    i1 = jnp.min(jnp.where(logits == m1, lane, LANES), axis=-1, keepdims=True)
    rest = jnp.where(lane == i1, -jnp.inf, logits)
    m2 = jnp.max(rest, axis=-1, keepdims=True)
    i2 = jnp.min(jnp.where(rest == m2, lane, LANES), axis=-1, keepdims=True)
    e2 = jnp.exp(m2 - m1)
    w1 = 1.0 / (1.0 + e2)
    w2 = e2 / (1.0 + e2)
    return jnp.where(lane == i1, w1, 0.0) + jnp.where(lane == i2, w2, 0.0)


def _moe_kernel(x_ref, g_ref, rw_ref, rb_ref, wg_ref, wu_ref, wd_ref, gf_ref, o_ref,
                h_ref, comb_ref, acc_ref, *, final_norm):
    e = pl.program_id(1)
    j = pl.program_id(2)

    @pl.when((e == 0) & (j == 0))
    def _():
        h32 = _rms(x_ref[...], g_ref[...])
        h_ref[...] = h32.astype(BF16)
        comb_ref[...] = _route(h32, rw_ref, rb_ref)
        acc_ref[...] = jnp.zeros_like(acc_ref)

    h = h_ref[...]
    comb = comb_ref[...]
    lane = lax.broadcasted_iota(jnp.int32, comb.shape, 1)
    ce = jnp.sum(jnp.where(lane == e, comb, 0.0), axis=-1, keepdims=True)
    gate = _dot(h, wg_ref[0])
    act = (gate * _sigmoid(gate) * _dot(h, wu_ref[0])).astype(BF16)
    acc_ref[...] += ce * _dot(act, wd_ref[0])

    @pl.when((e == pl.num_programs(1) - 1) & (j == pl.num_programs(2) - 1))
    def _():
        y = x_ref[...] + acc_ref[...]
        if final_norm:
            y = _rms(y, gf_ref[...])
        o_ref[...] = y


def _moe_call(x, g, rw, rb, w_in, w_down, g_final, final_norm, tm, tf):
    m, d = x.shape
    ne, fe, _ = w_down.shape
    nf = fe // tf
    row = lambda i, e, j: (i, 0)
    fix = lambda i, e, j: (0, 0)
    vmem = (4 * tm * d * 4 + tm * d * 2 + tm * d * 4 + tm * LANES * 4 + 2 * 3 * d * tf * 2
            + 2 * d * LANES * 4 + 4 * tm * tf * 4 + 2 * tm * d * 4 + (8 << 20))
    return pl.pallas_call(
        functools.partial(_moe_kernel, final_norm=final_norm),
        grid=(m // tm, ne, nf),
        in_specs=[pl.BlockSpec((tm, d), row), _resident((1, d), fix),
                  _resident((d, LANES), fix), _resident((1, LANES), fix),
                  pl.BlockSpec((1, d, tf), lambda i, e, j: (e, 0, j)),
                  pl.BlockSpec((1, d, tf), lambda i, e, j: (e, 0, j + nf)),
                  pl.BlockSpec((1, tf, d), lambda i, e, j: (e, j, 0)),
                  _resident((1, d), fix)],
        out_specs=pl.BlockSpec((tm, d), row),
        out_shape=jax.ShapeDtypeStruct((m, d), F32),
        scratch_shapes=[pltpu.VMEM((tm, d), BF16), pltpu.VMEM((tm, LANES), F32), pltpu.VMEM((tm, d), F32)],
        compiler_params=_cparams(("parallel", "arbitrary", "arbitrary"), vmem),
        name="moe_ffn",
    )(x, g, rw, rb, w_in, w_in, w_down, g_final)


def _rope_tables(n):
    rows = n // GRID_W
    axis_dim = HEAD_DIM // 2
    row = jnp.repeat(jnp.arange(rows, dtype=F32), GRID_W)
    col = jnp.tile(jnp.arange(GRID_W, dtype=F32), rows)
    inv = ROPE_THETA ** (-jnp.arange(0, axis_dim, 2, dtype=F32) / axis_dim)
    ang_r = row[:, None] * inv[None, :]
    ang_c = col[:, None] * inv[None, :]
    ang = jnp.concatenate([ang_r, ang_r, ang_c, ang_c], axis=-1)
    cos, sin = jnp.cos(ang), jnp.sin(ang)
    first_half = (jnp.arange(HEAD_DIM) % axis_dim) < (axis_dim // 2)
    sin_a = jnp.where(first_half[None, :], -sin, 0.0)
    sin_b = jnp.where(first_half[None, :], 0.0, sin)
    return cos, sin_a, sin_b


def _pick(total, pref):
    t = min(pref, total)
    while total % t:
        t //= 2
    return t


def kernel(x_prompt, x_sample, mem_prompt, mem_sample, norm_mix, w_in, a_q_norm, a_k_norm, r_decay, conv_w, conv_b, w_branch, w_out, norm_xattn, norm_mem, wq_x, wkv_x, wo_x, norm_ffn, ffn_w_in, ffn_w_down, router_w, router_b, moe_w_in, moe_w_down, norm_final):
    seq, d = x_prompt.shape[1], x_prompt.shape[2]
    n_mem = mem_prompt.shape[1]
    depth = norm_mix.shape[0]
    nb_p = x_prompt.shape[0]
    nseq = nb_p + x_sample.shape[0]
    x = jnp.concatenate([x_prompt.reshape(-1, d), x_sample.reshape(-1, d)], axis=0)
    mem = jnp.concatenate([mem_prompt.reshape(-1, d), mem_sample.reshape(-1, d)], axis=0)
    m = x.shape[0]

    tm = _pick(seq, 512)
    tq = _pick(seq, 256)
    tk = _pick(seq // 2, 512)
    cos, sin_a, sin_b = _rope_tables(seq)
    row2 = lambda v: v.reshape(1, -1).astype(F32)

    n_qk = (A_Q_HEADS + 2 * A_KV_HEADS) * HEAD_DIM + 2 * R_HEADS * R_QK_DIM
    n_vgc = 2 * R_HEADS * R_V_DIM + 3 * BRANCH_WIDTH
    log_g = jax.nn.log_sigmoid(r_decay.astype(F32))

    for l in range(depth):
        wl = w_in[l]
        w1 = wl[:, :n_qk].astype(BF16)
        w2 = wl[:, n_qk:n_qk + n_vgc].astype(BF16)
        wg = wl[:, n_qk + n_vgc:].reshape(d, N_BRANCH, d).transpose(1, 0, 2).astype(BF16)
        g_mix = row2(norm_mix[l])

        qa, ka, va, rq, rk = _p1_call(x, g_mix, w1, cos, sin_a, sin_b, row2(a_q_norm[l]),
                                      row2(a_k_norm[l]), seq, tm)
        rv, sg, cb, u = _p2_call(x, g_mix, w2, _pick(seq, 256))
        a = _attn_call(qa, ka, va, nseq, seq, tq, tk)
        r = _ret_call(log_g[l], rq, rk, rv, sg, nseq, seq)
        cw = jnp.zeros((SUBLANES, BRANCH_WIDTH), F32).at[:conv_w.shape[1]].set(conv_w[l].astype(F32))
        c = _conv_call(u, cb, cw, row2(conv_b[l]), seq, _pick(seq, 1024))
        merged = _merge_call(x, g_mix, wg, w_branch[l].astype(BF16), a, r, c, tm, _pick(d, 512))

        kv = _xkv_call(mem, row2(norm_mem[l]), wkv_x[l].astype(BF16), n_mem)
        x = _outx_call(x, merged, w_out[l].astype(BF16), row2(norm_xattn[l]), wq_x[l].astype(BF16),
                       kv, wo_x[l].astype(BF16), seq, n_mem, tm)

        last = l == depth - 1
        g_ffn = row2(norm_ffn[l])
        g_fin = row2(norm_final)
        if l % 2 == 0:
            x = _ffn_call(x, g_ffn, ffn_w_in[l // 2].astype(BF16), ffn_w_down[l // 2].astype(BF16),
                          g_fin, last, tm, 512)
        else:
            rw = jnp.zeros((d, LANES), F32).at[:, :N_EXPERTS].set(router_w[l // 2].astype(F32))
            rb = jnp.full((1, LANES), -jnp.inf, F32).at[0, :N_EXPERTS].set(router_b[l // 2].astype(F32))
            x = _moe_call(x, g_ffn, rw, rb, moe_w_in[l // 2].astype(BF16), moe_w_down[l // 2].astype(BF16),
                          g_fin, last, tm, 256)

    split = nb_p * seq
    return (x[:split].reshape(x_prompt.shape), x[split:].reshape(x_sample.shape))
```
